```python
import jax, jax.numpy as jnp
from jax import lax
import numpy as np

D_MODEL = 2048
BATCH = 8
SEQ = 2048
DEPTH = 1

GRID_W = 64
CTX_LEN = 256
RET_HEADS = 8
RET_DK = 128
RET_DV = 256
RET_CHUNK = 128
RET_QK_W = RET_HEADS * RET_DK
RET_V_W = RET_HEADS * RET_DV
MLA_HEADS = 16
MLA_Q_RANK = 512
MLA_KV_RANK = 512
MLA_D_NOPE = 128
MLA_D_ROPE = 64
MLA_D_V = 128
MLA_V_W = MLA_HEADS * MLA_D_V
Q_BLOCK = 128
FFN_DIM = 5632
CONV_W = 3
ROPE_BASE = 10000.0
EPS = 1e-6
IN_SIZES = (RET_QK_W, RET_V_W, MLA_KV_RANK, MLA_D_ROPE,
            RET_QK_W, RET_V_W, MLA_Q_RANK, D_MODEL, D_MODEL)
CTX_COLS = RET_QK_W + RET_V_W + MLA_KV_RANK + MLA_D_ROPE
IN_COLS = CTX_COLS + RET_QK_W + RET_V_W + MLA_Q_RANK + 2 * D_MODEL

kernel_name = "hybrid_retention_mla_convffn_dit"


def rms_norm(t):
    tf = t.astype(jnp.float32)
    return (tf * lax.rsqrt(jnp.mean(tf * tf, axis=-1, keepdims=True) + EPS)).astype(t.dtype)


def split_cols(t, sizes):
    out, start = [], 0
    for s in sizes:
        out.append(t[..., start:start + s])
        start += s
    return out


def axial_rope(t, row, col):
    dr = t.shape[-1]
    q4 = dr // 4
    inv = ROPE_BASE ** (-jnp.arange(q4, dtype=jnp.float32) / q4)
    ang = jnp.stack([row[:, None] * inv, col[:, None] * inv], axis=1)
    ang = ang.reshape((t.shape[1],) + (1,) * (t.ndim - 3) + (2, q4))
    cos, sin = jnp.cos(ang).astype(t.dtype), jnp.sin(ang).astype(t.dtype)
    tr = t.reshape(t.shape[:-1] + (2, 2, q4))
    t1, t2 = tr[..., 0, :], tr[..., 1, :]
    return jnp.stack([t1 * cos - t2 * sin, t2 * cos + t1 * sin], axis=-2).reshape(t.shape)


def retention_scan(q, k, v, r0, log_gamma):
    b, L, h, _ = q.shape
    dv = v.shape[-1]
    n = L // RET_CHUNK
    pos = jnp.arange(RET_CHUNK, dtype=jnp.float32)
    diff = pos[:, None] - pos[None, :]
    lower = diff >= 0
    inner = jnp.where(lower[None], jnp.exp(jnp.where(lower, diff, 0.0)[None] * log_gamma[:, None, None]), 0.0)
    q_dec = jnp.exp((pos + 1.0)[:, None] * log_gamma[None, :])
    k_dec = jnp.exp((RET_CHUNK - 1.0 - pos)[:, None] * log_gamma[None, :])
    chunk_dec = jnp.exp(RET_CHUNK * log_gamma)

    def to_chunks(t):
        return t.reshape(b, n, RET_CHUNK, h, t.shape[-1]).transpose(1, 0, 2, 3, 4)

    def step(r, qkv):
        qc, kc, vc = qkv
        s = jnp.einsum('bihd,bjhd->bhij', qc, kc) * inner
        o = jnp.einsum('bhij,bjhe->bihe', s, vc) + jnp.einsum('bihd,bhde->bihe', qc * q_dec[:, :, None], r)
        r = r * chunk_dec[:, None, None] + jnp.einsum('bjhd,bjhe->bhde', kc * k_dec[:, :, None], vc)
        return r, o

    _, o = lax.scan(step, r0, (to_chunks(q), to_chunks(k), to_chunks(v)))
    return o.transpose(1, 0, 2, 3, 4).reshape(b, L, h, dv)


def mla_attention(q_nope, q_rope, k_nope, k_rope, v):
    b, s, h, dn = q_nope.shape
    dr = q_rope.shape[-1]
    nb = s // Q_BLOCK
    scale = (dn + dr) ** -0.5

    def block(args):
        qn, qr = args
        sc = jnp.einsum('bqhd,bkhd->bhqk', qn, k_nope) + jnp.einsum('bqhr,bkr->bhqk', qr, k_rope)
        p = jax.nn.softmax(sc.astype(jnp.float32) * scale, axis=-1).astype(v.dtype)
        return jnp.einsum('bhqk,bkhe->bqhe', p, v)

    qn_b = q_nope.reshape(b, nb, Q_BLOCK, h, dn).transpose(1, 0, 2, 3, 4)
    qr_b = q_rope.reshape(b, nb, Q_BLOCK, h, dr).transpose(1, 0, 2, 3, 4)
    o = lax.map(block, (qn_b, qr_b))
    return o.transpose(1, 0, 2, 3, 4).reshape(b, s, h * v.shape[-1])


def dwconv_centred(t, w, bias):
    L = t.shape[1]
    p = CONV_W // 2
    tp = jnp.pad(t, ((0, 0), (p, p), (0, 0)))
    out = bias
    for i in range(CONV_W):
        out = out + tp[:, i:i + L] * w[i]
    return out


def setup_inputs(seed: int = 0) -> dict:
    key = jax.random.key(seed)
    ks = jax.random.split(key, 24)

    def nrm(k, shape, s):
        return jax.random.normal(k, shape, jnp.float32) * s

    gam = 1.0 - 2.0 ** (-5.0 - np.arange(RET_HEADS))
    logit = jnp.asarray(np.log(gam / (1.0 - gam)), dtype=jnp.float32)
    return {
        "x": nrm(ks[0], (BATCH, SEQ, D_MODEL), 1.0),
        "c": nrm(ks[1], (BATCH, D_MODEL), 1.0),
        "ctx": nrm(ks[2], (BATCH, CTX_LEN, D_MODEL), 1.0),
        "c_ctx": nrm(ks[3], (D_MODEL,), 1.0),
        "w_ada": nrm(ks[4], (DEPTH, D_MODEL, 6 * D_MODEL), 0.5 * D_MODEL ** -0.5),
        "b_ada": nrm(ks[5], (DEPTH, 6 * D_MODEL), 0.01),
        "w_in": nrm(ks[6], (DEPTH, D_MODEL, IN_COLS), D_MODEL ** -0.5),
        "ret_decay_fwd": logit[None] + nrm(ks[7], (DEPTH, RET_HEADS), 0.05),
        "ret_decay_bwd": logit[None] + nrm(ks[8], (DEPTH, RET_HEADS), 0.05),
        "ret_gn": 1.0 + nrm(ks[9], (DEPTH, RET_V_W), 0.02),
        "w_ret_o": nrm(ks[10], (DEPTH, RET_V_W, D_MODEL), RET_V_W ** -0.5),
        "mla_q_norm": 1.0 + nrm(ks[11], (DEPTH, MLA_Q_RANK), 0.02),
        "w_q_up": nrm(ks[12], (DEPTH, MLA_Q_RANK, MLA_HEADS * (MLA_D_NOPE + MLA_D_ROPE)), MLA_Q_RANK ** -0.5),
        "mla_kv_norm": 1.0 + nrm(ks[13], (DEPTH, MLA_KV_RANK), 0.02),
        "w_kv_up": nrm(ks[14], (DEPTH, MLA_KV_RANK, MLA_HEADS * (MLA_D_NOPE + MLA_D_V)), MLA_KV_RANK ** -0.5),
        "w_mla_o": nrm(ks[15], (DEPTH, MLA_V_W, D_MODEL), MLA_V_W ** -0.5),
        "w_out": nrm(ks[16], (DEPTH, D_MODEL, D_MODEL), D_MODEL ** -0.5),
        "ffn_w_up": nrm(ks[17], (DEPTH, D_MODEL, 2 * FFN_DIM), D_MODEL ** -0.5),
        "ffn_conv_w": nrm(ks[18], (DEPTH, CONV_W, FFN_DIM), CONV_W ** -0.5),
        "ffn_conv_b": nrm(ks[19], (DEPTH, FFN_DIM), 0.01),
        "ffn_w_down": nrm(ks[20], (DEPTH, FFN_DIM, D_MODEL), FFN_DIM ** -0.5),
        "final_norm": 1.0 + nrm(ks[21], (D_MODEL,), 0.02),
    }


def reference(x, c, ctx, c_ctx, w_ada, b_ada, w_in, ret_decay_fwd, ret_decay_bwd, ret_gn, w_ret_o,
              mla_q_norm, w_q_up, mla_kv_norm, w_kv_up, w_mla_o, w_out,
              ffn_w_up, ffn_conv_w, ffn_conv_b, ffn_w_down, final_norm):
    b, s, d = x.shape
    lc = ctx.shape[1]
    rows = s // GRID_W
    row = jnp.repeat(jnp.arange(rows, dtype=jnp.float32), GRID_W)
    col = jnp.tile(jnp.arange(GRID_W, dtype=jnp.float32), rows)
    pos_c = jnp.arange(lc, dtype=jnp.float32)

    for l in range(DEPTH):
        mod = jax.nn.silu(c) @ w_ada[l] + b_ada[l]
        sh1, sc1, g1, sh2, sc2, g2 = jnp.split(mod, 6, axis=-1)
        mod_c = jax.nn.silu(c_ctx) @ w_ada[l][:, :2 * d] + b_ada[l][:2 * d]
        sh1c, sc1c = mod_c[:d], mod_c[d:]

        h = rms_norm(x) * (1.0 + sc1[:, None]) + sh1[:, None]
        hc = rms_norm(ctx) * (1.0 + sc1c) + sh1c
        rk, rv, kvd, kr, rq, rg, qd, gate_ret, gate_mla = split_cols(h @ w_in[l], IN_SIZES)
        rkc, rvc, kvdc, krc = split_cols(hc @ w_in[l][:, :CTX_COLS], IN_SIZES[:4])

        lg_f = jax.nn.log_sigmoid(ret_decay_fwd[l].astype(jnp.float32))
        lg_b = jax.nn.log_sigmoid(ret_decay_bwd[l].astype(jnp.float32))
        q_r = axial_rope(rq.reshape(b, s, RET_HEADS, RET_DK), row, col).astype(jnp.float32)
        k_r = (axial_rope(rk.reshape(b, s, RET_HEADS, RET_DK), row, col) * RET_DK ** -0.5).astype(jnp.float32)
        v_r = rv.reshape(b, s, RET_HEADS, RET_DV).astype(jnp.float32)
        k_rc = (rkc.reshape(b, lc, RET_HEADS, RET_DK) * RET_DK ** -0.5).astype(jnp.float32)
        v_rc = rvc.reshape(b, lc, RET_HEADS, RET_DV).astype(jnp.float32)
        w_cf = jnp.exp((lc - 1.0 - pos_c)[:, None] * lg_f[None])
        w_cb = jnp.exp(pos_c[:, None] * lg_b[None])
        r_ctx_f = jnp.einsum('bjhd,bjhe->bhde', k_rc * w_cf[:, :, None], v_rc)
        r_ctx_b = jnp.einsum('bjhd,bjhe->bhde', k_rc * w_cb[:, :, None], v_rc)
        o_f = retention_scan(q_r, k_r, v_r, r_ctx_f, lg_f)
        o_b = retention_scan(q_r[:, ::-1], k_r[:, ::-1], v_r[:, ::-1], r_ctx_b, lg_b)[:, ::-1]
        o_r = o_f + o_b
        o_r = o_r * lax.rsqrt(jnp.mean(o_r * o_r, axis=-1, keepdims=True) + EPS)
        o_r = o_r.reshape(b, s, RET_V_W).astype(x.dtype) * ret_gn[l] * jax.nn.silu(rg)
        ret_branch = o_r @ w_ret_o[l]

        q_m = (rms_norm(qd) * mla_q_norm[l]) @ w_q_up[l]
        q_m = q_m.reshape(b, s, MLA_HEADS, MLA_D_NOPE + MLA_D_ROPE)
        q_nope = q_m[..., :MLA_D_NOPE]
        q_rope = axial_rope(q_m[..., MLA_D_NOPE:], row, col)
        kv = ((rms_norm(kvd) * mla_kv_norm[l]) @ w_kv_up[l]).reshape(b, s, MLA_HEADS, MLA_D_NOPE + MLA_D_V)
        kvc = ((rms_norm(kvdc) * mla_kv_norm[l]) @ w_kv_up[l]).reshape(b, lc, MLA_HEADS, MLA_D_NOPE + MLA_D_V)
        k_rope = axial_rope(kr, row, col)
        k_nope_all = jnp.concatenate([kvc[..., :MLA_D_NOPE], kv[..., :MLA_D_NOPE]], axis=1)
        v_all = jnp.concatenate([kvc[..., MLA_D_NOPE:], kv[..., MLA_D_NOPE:]], axis=1)
        k_rope_all = jnp.concatenate([krc, k_rope], axis=1)
        mla_branch = mla_attention(q_nope, q_rope, k_nope_all, k_rope_all, v_all) @ w_mla_o[l]

        mixed = jax.nn.sigmoid(gate_ret) * ret_branch + jax.nn.sigmoid(gate_mla) * mla_branch
        x = x + g1[:, None] * (mixed @ w_out[l])

        h2 = rms_norm(x) * (1.0 + sc2[:, None]) + sh2[:, None]
        up = h2 @ ffn_w_up[l]
        a, v_f = up[..., :FFN_DIM], up[..., FFN_DIM:]
        a = dwconv_centred(a, ffn_conv_w[l], ffn_conv_b[l])
        x = x + g2[:, None] * ((jax.nn.silu(a) * v_f) @ ffn_w_down[l])

    return rms_norm(x) * final_norm
```

```python
import functools

import numpy as np
import jax
import jax.numpy as jnp
from jax import lax
from jax.experimental import pallas as pl
from jax.experimental.pallas import tpu as pltpu

BF16 = jnp.bfloat16
F32 = jnp.float32

EPS = 1e-6
ROPE_BASE = 10000.0
GRID_W = 64
RET_HEADS = 8
RET_DK = 128
RET_DV = 256
RET_QK_W = RET_HEADS * RET_DK
RET_V_W = RET_HEADS * RET_DV
MLA_HEADS = 16
MLA_Q_RANK = 512
MLA_KV_RANK = 512
MLA_D_NOPE = 128
MLA_D_ROPE = 64
MLA_D_V = 128

LANES = 128
HEAD_QK_W = 2 * LANES
RET_CHUNK = 256
VMEM_LIMIT_BYTES = 56 * 1024 * 1024

NT_DIMS = (((1,), (1,)), ((), ()))
TN_DIMS = (((0,), (0,)), ((), ()))


def _params(n_grid_dims):
    return pltpu.CompilerParams(
        dimension_semantics=("arbitrary",) * n_grid_dims, vmem_limit_bytes=VMEM_LIMIT_BYTES)


def _tile(n, pref):
    t = min(n, pref)
    while n % t:
        t //= 2
    return t


def _sigmoid(v):
    return 1.0 / (1.0 + jnp.exp(-v))


def _rms(v):
    return v * lax.rsqrt(jnp.mean(v * v, axis=-1, keepdims=True) + EPS)


def _rope(t, cos, sin_lo, sin_hi, half):
    return t * cos + pltpu.roll(t, LANES - half, 1) * sin_lo + pltpu.roll(t, half, 1) * sin_hi


def _rope_tables(s, dr):
    q4 = dr // 4
    t = np.arange(s)
    inv = ROPE_BASE ** (-np.arange(q4, dtype=np.float64) / q4)
    ang_r = (t // GRID_W)[:, None] * inv
    ang_c = (t % GRID_W)[:, None] * inv
    zeros = np.zeros_like(ang_r)
    pad = np.zeros((s, LANES - dr))
    cos = np.concatenate([np.cos(ang_r), np.cos(ang_r), np.cos(ang_c), np.cos(ang_c), pad], axis=1)
    sin_lo = np.concatenate([-np.sin(ang_r), zeros, -np.sin(ang_c), zeros, pad], axis=1)
    sin_hi = np.concatenate([zeros, np.sin(ang_r), zeros, np.sin(ang_c), pad], axis=1)
    return tuple(jnp.asarray(a, F32) for a in (cos, sin_lo, sin_hi))


def _ada_kernel(c_ref, w_ref, b_ref, o_ref):
    c = c_ref[...]
    s = (c * _sigmoid(c)).astype(BF16)
    o_ref[...] = jnp.dot(s, w_ref[...].astype(BF16), preferred_element_type=F32) + b_ref[...]


def _ada(cs, w, b):
    r, d = cs.shape
    n = w.shape[1]
    tn = _tile(n, 1024)
    return pl.pallas_call(
        _ada_kernel,
        out_shape=jax.ShapeDtypeStruct((r, n), F32),
        grid=(n // tn,),
        in_specs=[pl.BlockSpec((r, d), lambda j: (0, 0)),
                  pl.BlockSpec((d, tn), lambda j: (0, j)),
                  pl.BlockSpec((1, tn), lambda j: (0, j))],
        out_specs=pl.BlockSpec((r, tn), lambda j: (0, j)),
        compiler_params=_params(1), name="ada")(cs, w, b)


def _in_proj_kernel(x_ref, sh_ref, sc_ref, w_ref, wkr_ref, p_ref, kr_ref, h_ref, *, row_chunk):
    @pl.when(pl.program_id(1) == 0)
    def _():
        scale = 1.0 + sc_ref[0]
        shift = sh_ref[0]

        def body(i, carry):
            r = pl.ds(pl.multiple_of(i * row_chunk, row_chunk), row_chunk)
            h_ref[r, :] = (_rms(x_ref[r, :]) * scale + shift).astype(BF16)
            return carry

        lax.fori_loop(0, h_ref.shape[0] // row_chunk, body, 0)
        kr_ref[...] = jnp.dot(h_ref[...], wkr_ref[...], preferred_element_type=F32).astype(BF16)

    p_ref[...] = jnp.dot(h_ref[...], w_ref[...], preferred_element_type=F32).astype(BF16)


def _in_proj(x2, mod3, first_mod_row, rows_per_mod, w, w_kr, n_cols, name):
    m, d = x2.shape
    tm = _tile(rows_per_mod, 1024)
    tn = _tile(n_cols, 1024)
    mod_idx = lambda j: (lambda i, n: (first_mod_row + i * tm // rows_per_mod, 0, j))
    return pl.pallas_call(
        functools.partial(_in_proj_kernel, row_chunk=_tile(tm, 256)),
        out_shape=(jax.ShapeDtypeStruct((m, n_cols), BF16), jax.ShapeDtypeStruct((m, LANES), BF16)),
        grid=(m // tm, n_cols // tn),
        in_specs=[pl.BlockSpec((tm, d), lambda i, n: (i, 0)),
                  pl.BlockSpec((1, 1, d), mod_idx(0)),
                  pl.BlockSpec((1, 1, d), mod_idx(1)),
                  pl.BlockSpec((d, tn), lambda i, n: (0, n)),
                  pl.BlockSpec((d, LANES), lambda i, n: (0, 0))],
        out_specs=(pl.BlockSpec((tm, tn), lambda i, n: (i, n)),
                   pl.BlockSpec((tm, LANES), lambda i, n: (i, 0))),
        scratch_shapes=[pltpu.VMEM((tm, d), BF16)],
        compiler_params=_params(2), name=name)(x2, mod3, mod3, w, w_kr)


def _log_sigmoid(v):
    return jnp.minimum(v, 0.0) - jnp.log1p(jnp.exp(-jnp.abs(v)))


def _retention_kernel(q_ref, k_ref, v_ref, g_ref, kc_ref, vc_ref, cos_ref, slo_ref, shi_ref,
                      df_ref, db_ref, gn_ref, o_ref, kv_ref, st_ref, kr_ref, *, seq, ctx_len, chunk):
    n_chunks = seq // chunk
    lg_f = _log_sigmoid(df_ref[0])[0:1, :]
    lg_b = _log_sigmoid(db_ref[0])[0:1, :]
    lg_f_k, lg_b_k = lg_f[:, :RET_DK], lg_b[:, :RET_DK]

    def key_decay(length):
        j = lax.broadcasted_iota(jnp.int32, (length, RET_DK), 0).astype(F32)
        return jnp.exp((length - 1.0 - j) * lg_f_k), jnp.exp(j * lg_b_k)

    def state_update(kf, kd_f, kd_b, v):
        kk = jnp.concatenate([kf * kd_f, kf * kd_b], axis=1).astype(BF16)
        return lax.dot_general(kk, v, TN_DIMS, preferred_element_type=F32)

    kd_f, kd_b = key_decay(chunk)

    def pass1(c, carry):
        r = pl.ds(pl.multiple_of(c * chunk, chunk), chunk)
        kf = _rope(k_ref[r, :].astype(F32), cos_ref[r, :], slo_ref[r, :], shi_ref[r, :], RET_DK // 4)
        kr_ref[r, :] = kf
        kv_ref[c] = state_update(kf, kd_f, kd_b, v_ref[r, :])
        return carry

    lax.fori_loop(0, n_chunks, pass1, 0)

    cd_f, cd_b = key_decay(ctx_len)
    ctx_state = state_update(kc_ref[...].astype(F32), cd_f, cd_b, vc_ref[...])
    g_chunk_f = jnp.exp(chunk * lg_f)
    g_chunk_b = jnp.exp(chunk * lg_b)
    r_f = ctx_state[:RET_DK]
    for c in range(n_chunks):
        st_ref[c, :RET_DK, :] = r_f.astype(BF16)
        r_f = r_f * g_chunk_f + kv_ref[c, :RET_DK, :]
    r_b = ctx_state[RET_DK:]
    for c in reversed(range(n_chunks)):
        st_ref[c, RET_DK:, :] = r_b.astype(BF16)
        r_b = r_b * g_chunk_b + kv_ref[c, RET_DK:, :]

    i = lax.broadcasted_iota(jnp.int32, (chunk, chunk), 0)
    j = lax.broadcasted_iota(jnp.int32, (chunk, chunk), 1)
    diff = (i - j).astype(F32)
    lower, upper = diff >= 0, diff <= 0
    decay = (jnp.where(lower, jnp.exp(jnp.where(lower, diff, 0.0) * lg_f), 0.0)
             + jnp.where(upper, jnp.exp(jnp.where(upper, -diff, 0.0) * lg_b), 0.0))
    qi = lax.broadcasted_iota(jnp.int32, (chunk, RET_DK), 0).astype(F32)
    qd_f = jnp.exp((qi + 1.0) * lg_f_k)
    qd_b = jnp.exp((chunk - qi) * lg_b_k)
    gain = gn_ref[...]

    def pass2(c, carry):
        r = pl.ds(pl.multiple_of(c * chunk, chunk), chunk)
        qf = _rope(q_ref[r, :].astype(F32), cos_ref[r, :], slo_ref[r, :], shi_ref[r, :], RET_DK // 4)
        qf = qf * RET_DK ** -0.5
        s = lax.dot_general(qf.astype(BF16), kr_ref[r, :].astype(BF16), NT_DIMS,
                            preferred_element_type=F32) * decay
        o = jnp.dot(s.astype(BF16), v_ref[r, :], preferred_element_type=F32)
        qq = jnp.concatenate([qf * qd_f, qf * qd_b], axis=1).astype(BF16)
        o = o + jnp.dot(qq, st_ref[c], preferred_element_type=F32)
        gate = g_ref[r, :].astype(F32)
        o_ref[r, :] = (_rms(o) * gain * (gate * _sigmoid(gate))).astype(BF16)
        return carry

    lax.fori_loop(0, n_chunks, pass2, 0)


def _retention(p, pc, tabs, dec_f, dec_b, gn, batch, seq, ctx_len, col):
    chunk = RET_CHUNK
    assert seq % chunk == 0 and col["rk"] % RET_DK == 0 and col["rq"] % RET_DK == 0
    assert col["rv"] % RET_DV == 0 and col["rg"] % RET_DV == 0
    kq, kk, kv, kg = col["rq"] // RET_DK, col["rk"] // RET_DK, col["rv"] // RET_DV, col["rg"] // RET_DV
    tab_spec = pl.BlockSpec((seq, LANES), lambda b, h: (0, 0))
    dec_spec = pl.BlockSpec((1, 8, chunk), lambda b, h: (h, 0, 0))
    return pl.pallas_call(
        functools.partial(_retention_kernel, seq=seq, ctx_len=ctx_len, chunk=chunk),
        out_shape=jax.ShapeDtypeStruct((batch * seq, RET_V_W), BF16),
        grid=(batch, RET_HEADS),
        in_specs=[pl.BlockSpec((seq, RET_DK), lambda b, h: (b, kq + h)),
                  pl.BlockSpec((seq, RET_DK), lambda b, h: (b, kk + h)),
                  pl.BlockSpec((seq, RET_DV), lambda b, h: (b, kv + h)),
                  pl.BlockSpec((seq, RET_DV), lambda b, h: (b, kg + h)),
                  pl.BlockSpec((ctx_len, RET_DK), lambda b, h: (b, kk + h)),
                  pl.BlockSpec((ctx_len, RET_DV), lambda b, h: (b, kv + h)),
                  tab_spec, tab_spec, tab_spec, dec_spec, dec_spec,
                  pl.BlockSpec((1, RET_DV), lambda b, h: (0, h))],
        out_specs=pl.BlockSpec((seq, RET_DV), lambda b, h: (b, h)),
        scratch_shapes=[pltpu.VMEM((seq // chunk, 2 * RET_DK, RET_DV), F32),
                        pltpu.VMEM((seq // chunk, 2 * RET_DK, RET_DV), BF16),
                        pltpu.VMEM((seq, RET_DK), F32)],
        compiler_params=_params(2), name="retention")(p, p, p, p, pc, pc, *tabs, dec_f, dec_b, gn)


def _q_prep_kernel(qd_ref, nrm_ref, w_ref, cos_ref, slo_ref, shi_ref, o_ref, *, scale):
    qn = (_rms(qd_ref[...].astype(F32)) * nrm_ref[...]).astype(BF16)
    cos, slo, shi = cos_ref[...], slo_ref[...], shi_ref[...]
    for h in range(MLA_HEADS):
        lo = h * HEAD_QK_W
        acc = jnp.dot(qn, w_ref[:, lo:lo + HEAD_QK_W], preferred_element_type=F32)
        o_ref[:, lo:lo + LANES] = (acc[:, :LANES] * scale).astype(BF16)
        rot = _rope(acc[:, LANES:], cos, slo, shi, MLA_D_ROPE // 4)
        o_ref[:, lo + LANES:lo + HEAD_QK_W] = (rot * scale).astype(BF16)


def _q_prep(p, nrm, wq, tabs, seq, col, scale):
    m = p.shape[0]
    tm = _tile(seq, 512)
    assert col["qd"] % MLA_Q_RANK == 0
    kq = col["qd"] // MLA_Q_RANK
    n = MLA_HEADS * HEAD_QK_W
    tab_spec = pl.BlockSpec((tm, LANES), lambda i: (i % (seq // tm), 0))
    return pl.pallas_call(
        functools.partial(_q_prep_kernel, scale=scale),
        out_shape=jax.ShapeDtypeStruct((m, n), BF16),
        grid=(m // tm,),
        in_specs=[pl.BlockSpec((tm, MLA_Q_RANK), lambda i: (i, kq)),
                  pl.BlockSpec((1, MLA_Q_RANK), lambda i: (0, 0)),
                  pl.BlockSpec((MLA_Q_RANK, n), lambda i: (0, 0)),
                  tab_spec, tab_spec, tab_spec],
        out_specs=pl.BlockSpec((tm, n), lambda i: (i, 0)),
        compiler_params=_params(1), name="q_prep")(p, nrm, wq, *tabs)


def _kv_prep_kernel(*refs, rope):
    if rope:
        kvd_ref, kr_ref, nrm_ref, wk_ref, wv_ref, cos_ref, slo_ref, shi_ref, k_ref, v_ref = refs
    else:
        kvd_ref, kr_ref, nrm_ref, wk_ref, wv_ref, k_ref, v_ref = refs
    kvn = (_rms(kvd_ref[...].astype(F32)) * nrm_ref[...]).astype(BF16)
    kr = kr_ref[...].astype(F32)
    if rope:
        kr = _rope(kr, cos_ref[...], slo_ref[...], shi_ref[...], MLA_D_ROPE // 4)
    kr = kr.astype(BF16)
    for pair in range(MLA_HEADS // 2):
        acc = jnp.dot(kvn, wk_ref[:, pair * 2 * LANES:(pair + 1) * 2 * LANES], preferred_element_type=F32)
        for half in range(2):
            lo = (2 * pair + half) * HEAD_QK_W
            k_ref[:, lo:lo + LANES] = acc[:, half * LANES:(half + 1) * LANES].astype(BF16)
            k_ref[:, lo + LANES:lo + HEAD_QK_W] = kr
    v_ref[...] = jnp.dot(kvn, wv_ref[...], preferred_element_type=F32).astype(BF16)


def _kv_prep(p, kr, nrm, wk, wv, tabs, seq, col, name):
    m = p.shape[0]
    tm = _tile(seq, 512)
    assert col["kvd"] % MLA_KV_RANK == 0
    kk = col["kvd"] // MLA_KV_RANK
    nk, nv = MLA_HEADS * HEAD_QK_W, MLA_HEADS * MLA_D_V
    in_specs = [pl.BlockSpec((tm, MLA_KV_RANK), lambda i: (i, kk)),
                pl.BlockSpec((tm, LANES), lambda i: (i, 0)),
                pl.BlockSpec((1, MLA_KV_RANK), lambda i: (0, 0)),
                pl.BlockSpec((MLA_KV_RANK, MLA_HEADS * MLA_D_NOPE), lambda i: (0, 0)),
                pl.BlockSpec((MLA_KV_RANK, nv), lambda i: (0, 0))]
    args = [p, kr, nrm, wk, wv]
    if tabs is not None:
        in_specs += [pl.BlockSpec((tm, LANES), lambda i: (i % (seq // tm), 0))] * 3
        args += list(tabs)
    return pl.pallas_call(
        functools.partial(_kv_prep_kernel, rope=tabs is not None),
        out_shape=(jax.ShapeDtypeStruct((m, nk), BF16), jax.ShapeDtypeStruct((m, nv), BF16)),
        grid=(m // tm,),
        in_specs=in_specs,
        out_specs=(pl.BlockSpec((tm, nk), lambda i: (i, 0)), pl.BlockSpec((tm, nv), lambda i: (i, 0))),
        compiler_params=_params(1), name=name)(*args)


def _attn_kernel(q_ref, kl_ref, kc_ref, vl_ref, vc_ref, o_ref):
    q = q_ref[...]
    s_lat = lax.dot_general(q, kl_ref[...], NT_DIMS, preferred_element_type=F32)
    s_ctx = lax.dot_general(q, kc_ref[...], NT_DIMS, preferred_element_type=F32)
    mx = jnp.maximum(jnp.max(s_lat, axis=-1, keepdims=True), jnp.max(s_ctx, axis=-1, keepdims=True))
    p_lat = jnp.exp2(s_lat - mx)
    p_ctx = jnp.exp2(s_ctx - mx)
    denom = jnp.sum(p_lat, axis=-1, keepdims=True) + jnp.sum(p_ctx, axis=-1, keepdims=True)
    o = (jnp.dot(p_lat.astype(BF16), vl_ref[...], preferred_element_type=F32)
         + jnp.dot(p_ctx.astype(BF16), vc_ref[...], preferred_element_type=F32))
    o_ref[...] = (o / denom).astype(BF16)


def _attention(q, k_lat, k_ctx, v_lat, v_ctx, batch, seq, ctx_len):
    tq = _tile(seq, 512)
    nq = seq // tq
    return pl.pallas_call(
        _attn_kernel,
        out_shape=jax.ShapeDtypeStruct((batch * seq, MLA_HEADS * MLA_D_V), BF16),
        grid=(batch, MLA_HEADS, nq),
        in_specs=[pl.BlockSpec((tq, HEAD_QK_W), lambda b, h, i: (b * nq + i, h)),
                  pl.BlockSpec((seq, HEAD_QK_W), lambda b, h, i: (b, h)),
                  pl.BlockSpec((ctx_len, HEAD_QK_W), lambda b, h, i: (b, h)),
                  pl.BlockSpec((seq, MLA_D_V), lambda b, h, i: (b, h)),
                  pl.BlockSpec((ctx_len, MLA_D_V), lambda b, h, i: (b, h))],
        out_specs=pl.BlockSpec((tq, MLA_D_V), lambda b, h, i: (b * nq + i, h)),
        compiler_params=_params(3), name="attention")(q, k_lat, k_ctx, v_lat, v_ctx)


def _merge_kernel(ret_ref, att_ref, wr_ref, wa_ref, gr_ref, ga_ref, o_ref):
    r = jnp.dot(ret_ref[...], wr_ref[...], preferred_element_type=F32)
    a = jnp.dot(att_ref[...], wa_ref[...], preferred_element_type=F32)
    o_ref[...] = (_sigmoid(gr_ref[...].astype(F32)) * r + _sigmoid(ga_ref[...].astype(F32)) * a).astype(BF16)


def _merge(o_ret, o_att, w_ret, w_att, p, col):
    m, d = o_ret.shape
    n = w_ret.shape[1]
    tm, tn = _tile(m, 1024), _tile(n, 512)
    assert col["gate_ret"] % tn == 0 and col["gate_mla"] % tn == 0
    kr, ka = col["gate_ret"] // tn, col["gate_mla"] // tn
    return pl.pallas_call(
        _merge_kernel,
        out_shape=jax.ShapeDtypeStruct((m, n), BF16),
        grid=(m // tm, n // tn),
        in_specs=[pl.BlockSpec((tm, d), lambda i, j: (i, 0)),
                  pl.BlockSpec((tm, d), lambda i, j: (i, 0)),
                  pl.BlockSpec((d, tn), lambda i, j: (0, j)),
                  pl.BlockSpec((d, tn), lambda i, j: (0, j)),
                  pl.BlockSpec((tm, tn), lambda i, j: (i, kr + j)),
                  pl.BlockSpec((tm, tn), lambda i, j: (i, ka + j))],
        out_specs=pl.BlockSpec((tm, tn), lambda i, j: (i, j)),
        compiler_params=_params(2), name="merge")(o_ret, o_att, w_ret, w_att, p, p)


def _out_proj_kernel(mx_ref, w_ref, x_ref, g_ref, sh_ref, sc_ref, x1_ref, h2_ref):
    y = jnp.dot(mx_ref[...], w_ref[...], preferred_element_type=F32)
    x1 = x_ref[...] + g_ref[0] * y
    x1_ref[...] = x1
    h2_ref[...] = (_rms(x1) * (1.0 + sc_ref[0]) + sh_ref[0]).astype(BF16)


def _out_proj(mixed, w, x2, mod3, seq):
    m, d = x2.shape
    tm = _tile(seq, 512)
    mod_idx = lambda j: (lambda i: (i * tm // seq, 0, j))
    return pl.pallas_call(
        _out_proj_kernel,
        out_shape=(jax.ShapeDtypeStruct((m, d), F32), jax.ShapeDtypeStruct((m, d), BF16)),
        grid=(m // tm,),
        in_specs=[pl.BlockSpec((tm, d), lambda i: (i, 0)),
                  pl.BlockSpec((d, d), lambda i: (0, 0)),
                  pl.BlockSpec((tm, d), lambda i: (i, 0)),
                  pl.BlockSpec((1, 1, d), mod_idx(2)),
                  pl.BlockSpec((1, 1, d), mod_idx(3)),
                  pl.BlockSpec((1, 1, d), mod_idx(4))],
        out_specs=(pl.BlockSpec((tm, d), lambda i: (i, 0)), pl.BlockSpec((tm, d), lambda i: (i, 0))),
        compiler_params=_params(1), name="out_proj")(mixed, w, x2, mod3, mod3, mod3)


def _ffn_up_kernel(h_ref, wa_ref, wv_ref, cw_ref, cb_ref, o_ref):
    h = h_ref[...]
    a = jnp.dot(h, wa_ref[...], preferred_element_type=F32)
    v = jnp.dot(h, wv_ref[...], preferred_element_type=F32)
    rows = a.shape[0]
    row = lax.broadcasted_iota(jnp.int32, a.shape, 0)
    a_prev = jnp.where(row == 0, 0.0, pltpu.roll(a, 1, 0))
    a_next = jnp.where(row == rows - 1, 0.0, pltpu.roll(a, rows - 1, 0))
    cw = cw_ref[...]
    conv = cb_ref[...] + a_prev * cw[0:1] + a * cw[1:2] + a_next * cw[2:3]
    o_ref[...] = (conv * _sigmoid(conv) * v).astype(BF16)


def _ffn_up(h2, w_up, conv_w, conv_b, batch, seq):
    d = h2.shape[1]
    f = conv_w.shape[1]
    tn = _tile(f, 256)
    nf = f // tn
    return pl.pallas_call(
        _ffn_up_kernel,
        out_shape=jax.ShapeDtypeStruct((batch * seq, f), BF16),
        grid=(batch, nf),
        in_specs=[pl.BlockSpec((seq, d), lambda b, j: (b, 0)),
                  pl.BlockSpec((d, tn), lambda b, j: (0, j)),
                  pl.BlockSpec((d, tn), lambda b, j: (0, nf + j)),
                  pl.BlockSpec((conv_w.shape[0], tn), lambda b, j: (0, j)),
                  pl.BlockSpec((1, tn), lambda b, j: (0, j))],
        out_specs=pl.BlockSpec((seq, tn), lambda b, j: (b, j)),
        compiler_params=_params(2), name="ffn_up")(h2, w_up, w_up, conv_w, conv_b)


def _ffn_down_kernel(g_ref, w_ref, x1_ref, g2_ref, fn_ref, o_ref, acc_ref):
    k = pl.program_id(1)
    part = jnp.dot(g_ref[...], w_ref[...], preferred_element_type=F32)

    @pl.when(k == 0)
    def _():
        acc_ref[...] = part

    @pl.when(k > 0)
    def _():
        acc_ref[...] += part

    @pl.when(k == pl.num_programs(1) - 1)
    def _():
        x2 = x1_ref[...] + g2_ref[0] * acc_ref[...]
        o_ref[...] = _rms(x2) * fn_ref[...]


def _ffn_down(g, w, x1, mod3, fn, seq):
    m, f = g.shape
    d = w.shape[1]
    tm, tk = _tile(seq, 512), _tile(f, 512)
    return pl.pallas_call(
        _ffn_down_kernel,
        out_shape=jax.ShapeDtypeStruct((m, d), F32),
        grid=(m // tm, f // tk),
        in_specs=[pl.BlockSpec((tm, tk), lambda i, k: (i, k)),
                  pl.BlockSpec((tk, d), lambda i, k: (k, 0)),
                  pl.BlockSpec((tm, d), lambda i, k: (i, 0)),
                  pl.BlockSpec((1, 1, d), lambda i, k: (i * tm // seq, 0, 5)),
                  pl.BlockSpec((1, d), lambda i, k: (0, 0))],
        out_specs=pl.BlockSpec((tm, d), lambda i, k: (i, 0)),
        scratch_shapes=[pltpu.VMEM((tm, d), F32)],
        compiler_params=_params(2), name="ffn_down")(g, w, x1, mod3, fn)


def kernel(x, c, ctx, c_ctx, w_ada, b_ada, w_in, ret_decay_fwd, ret_decay_bwd, ret_gn, w_ret_o, mla_q_norm, w_q_up, mla_kv_norm, w_kv_up, w_mla_o, w_out, ffn_w_up, ffn_conv_w, ffn_conv_b, ffn_w_down, final_norm):
    b, s, d = x.shape
    lc = ctx.shape[1]
    assert w_ada.shape[0] == 1, "single trunk layer"
    assert s % GRID_W == 0

    n_mod = -(-(b + 1) // 8) * 8
    cs = jnp.concatenate([c, c_ctx[None], jnp.zeros((n_mod - b - 1, d), F32)], axis=0)
    mod3 = _ada(cs, w_ada[0], b_ada).reshape(n_mod, 1, 6 * d)

    src, off = {}, 0
    for name, width in zip(("rk", "rv", "kvd", "kr", "rq", "rg", "qd", "gate_ret", "gate_mla"),
                           (RET_QK_W, RET_V_W, MLA_KV_RANK, MLA_D_ROPE, RET_QK_W, RET_V_W, MLA_Q_RANK, d, d)):
        src[name] = (off, width)
        off += width
    order = ("rk", "rv", "kvd", "rq", "rg", "qd", "gate_ret", "gate_mla")
    col, off = {}, 0
    for name in order:
        col[name] = off
        off += src[name][1]
    n_ctx_cols = col["rq"]
    w_in0 = w_in[0]
    w_main = jnp.concatenate([w_in0[:, src[n][0]:src[n][0] + src[n][1]] for n in order], axis=1).astype(BF16)
    w_kr = jnp.pad(w_in0[:, src["kr"][0]:src["kr"][0] + MLA_D_ROPE], ((0, 0), (0, LANES - MLA_D_ROPE))).astype(BF16)

    p, kr = _in_proj(x.reshape(b * s, d), mod3, 0, s, w_main, w_kr, off, "in_proj")
    pc, krc = _in_proj(ctx.reshape(b * lc, d), mod3, b, b * lc, w_main, w_kr, n_ctx_cols, "in_proj_ctx")

    dec_f = jnp.broadcast_to(ret_decay_fwd[0].astype(F32)[:, None, None], (RET_HEADS, 8, RET_CHUNK))
    dec_b = jnp.broadcast_to(ret_decay_bwd[0].astype(F32)[:, None, None], (RET_HEADS, 8, RET_CHUNK))
    o_ret = _retention(p, pc, _rope_tables(s, RET_DK), dec_f, dec_b, ret_gn, b, s, lc, col)

    dqk = MLA_D_NOPE + MLA_D_ROPE
    wq3 = w_q_up[0].reshape(MLA_Q_RANK, MLA_HEADS, dqk)
    wq = jnp.pad(wq3, ((0, 0), (0, 0), (0, HEAD_QK_W - dqk))).reshape(MLA_Q_RANK, MLA_HEADS * HEAD_QK_W).astype(BF16)
    wkv3 = w_kv_up[0].reshape(MLA_KV_RANK, MLA_HEADS, MLA_D_NOPE + MLA_D_V)
    wk = wkv3[:, :, :MLA_D_NOPE].reshape(MLA_KV_RANK, MLA_HEADS * MLA_D_NOPE).astype(BF16)
    wv = wkv3[:, :, MLA_D_NOPE:].reshape(MLA_KV_RANK, MLA_HEADS * MLA_D_V).astype(BF16)
    tabs = _rope_tables(s, MLA_D_ROPE)
    q = _q_prep(p, mla_q_norm, wq, tabs, s, col, dqk ** -0.5 * float(np.log2(np.e)))
    k_lat, v_lat = _kv_prep(p, kr, mla_kv_norm, wk, wv, tabs, s, col, "kv_prep")
    k_ctx, v_ctx = _kv_prep(pc, krc, mla_kv_norm, wk, wv, None, lc, col, "kv_prep_ctx")
    o_att = _attention(q, k_lat, k_ctx, v_lat, v_ctx, b, s, lc)

    mixed = _merge(o_ret, o_att, w_ret_o[0].astype(BF16), w_mla_o[0].astype(BF16), p, col)
    x1, h2 = _out_proj(mixed, w_out[0].astype(BF16), x.reshape(b * s, d), mod3, s)

    g = _ffn_up(h2, ffn_w_up[0].astype(BF16), ffn_conv_w[0], ffn_conv_b, b, s)
    out = _ffn_down(g, ffn_w_down[0].astype(BF16), x1, mod3, final_norm[None], s)
    return out.reshape(b, s, d)
```

```python
import functools

import numpy as np
import jax
import jax.numpy as jnp
from jax import lax
from jax.experimental import pallas as pl
from jax.experimental.pallas import tpu as pltpu

BF16 = jnp.bfloat16
F32 = jnp.float32

EPS = 1e-6
ROPE_BASE = 10000.0
GRID_W = 64
RET_HEADS = 8
RET_DK = 128
RET_DV = 256
RET_QK_W = RET_HEADS * RET_DK
RET_V_W = RET_HEADS * RET_DV
MLA_HEADS = 16
MLA_Q_RANK = 512
MLA_KV_RANK = 512
MLA_D_NOPE = 128
MLA_D_ROPE = 64
MLA_D_V = 128

LANES = 128
HEAD_QK_W = 2 * LANES
RET_CHUNK = 256
VMEM_LIMIT_BYTES = 56 * 1024 * 1024

NT_DIMS = (((1,), (1,)), ((), ()))
TN_DIMS = (((0,), (0,)), ((), ()))


def _params(n_grid_dims):
    return pltpu.CompilerParams(
        dimension_semantics=("arbitrary",) * n_grid_dims, vmem_limit_bytes=VMEM_LIMIT_BYTES)


def _tile(n, pref):
    t = min(n, pref)
    while n % t:
        t //= 2
    return t


def _sigmoid(v):
    return 1.0 / (1.0 + jnp.exp(-v))


def _rms(v):
    return v * lax.rsqrt(jnp.mean(v * v, axis=-1, keepdims=True) + EPS)


def _rope(t, cos, sin_lo, sin_hi, half):
    return t * cos + pltpu.roll(t, LANES - half, 1) * sin_lo + pltpu.roll(t, half, 1) * sin_hi


def _rope_tables(s, dr):
    q4 = dr // 4
    t = np.arange(s)
    inv = ROPE_BASE ** (-np.arange(q4, dtype=np.float64) / q4)
    ang_r = (t // GRID_W)[:, None] * inv
    ang_c = (t % GRID_W)[:, None] * inv
    zeros = np.zeros_like(ang_r)
    pad = np.zeros((s, LANES - dr))
    cos = np.concatenate([np.cos(ang_r), np.cos(ang_r), np.cos(ang_c), np.cos(ang_c), pad], axis=1)
    sin_lo = np.concatenate([-np.sin(ang_r), zeros, -np.sin(ang_c), zeros, pad], axis=1)
    sin_hi = np.concatenate([zeros, np.sin(ang_r), zeros, np.sin(ang_c), pad], axis=1)
    return tuple(jnp.asarray(a, F32) for a in (cos, sin_lo, sin_hi))


def _ada_kernel(c_ref, w_ref, b_ref, o_ref):
    c = c_ref[...]
    s = (c * _sigmoid(c)).astype(BF16)
    o_ref[...] = jnp.dot(s, w_ref[...].astype(BF16), preferred_element_type=F32) + b_ref[...]


def _ada(cs, w, b):
    r, d = cs.shape
    n = w.shape[1]
    tn = _tile(n, 1024)
    return pl.pallas_call(
        _ada_kernel,
        out_shape=jax.ShapeDtypeStruct((r, n), F32),
        grid=(n // tn,),
        in_specs=[pl.BlockSpec((r, d), lambda j: (0, 0)),
                  pl.BlockSpec((d, tn), lambda j: (0, j)),
                  pl.BlockSpec((1, tn), lambda j: (0, j))],
        out_specs=pl.BlockSpec((r, tn), lambda j: (0, j)),
        compiler_params=_params(1), name="ada")(cs, w, b)


def _in_proj_kernel(x_ref, sh_ref, sc_ref, w_ref, wkr_ref, p_ref, kr_ref, h_ref, *, row_chunk):
    @pl.when(pl.program_id(1) == 0)
    def _():
        scale = 1.0 + sc_ref[0]
        shift = sh_ref[0]

        def body(i, carry):
            r = pl.ds(pl.multiple_of(i * row_chunk, row_chunk), row_chunk)
            h_ref[r, :] = (_rms(x_ref[r, :]) * scale + shift).astype(BF16)
            return carry

        lax.fori_loop(0, h_ref.shape[0] // row_chunk, body, 0)
        kr_ref[...] = jnp.dot(h_ref[...], wkr_ref[...], preferred_element_type=F32).astype(BF16)

    p_ref[...] = jnp.dot(h_ref[...], w_ref[...], preferred_element_type=F32).astype(BF16)


def _in_proj(x2, mod3, first_mod_row, rows_per_mod, w, w_kr, n_cols, name):
    m, d = x2.shape
    tm = _tile(rows_per_mod, 1024)
    tn = _tile(n_cols, 1024)
    mod_idx = lambda j: (lambda i, n: (first_mod_row + i * tm // rows_per_mod, 0, j))
    return pl.pallas_call(
        functools.partial(_in_proj_kernel, row_chunk=_tile(tm, 256)),
        out_shape=(jax.ShapeDtypeStruct((m, n_cols), BF16), jax.ShapeDtypeStruct((m, LANES), BF16)),
        grid=(m // tm, n_cols // tn),
        in_specs=[pl.BlockSpec((tm, d), lambda i, n: (i, 0)),
                  pl.BlockSpec((1, 1, d), mod_idx(0)),
                  pl.BlockSpec((1, 1, d), mod_idx(1)),
                  pl.BlockSpec((d, tn), lambda i, n: (0, n)),
                  pl.BlockSpec((d, LANES), lambda i, n: (0, 0))],
        out_specs=(pl.BlockSpec((tm, tn), lambda i, n: (i, n)),
                   pl.BlockSpec((tm, LANES), lambda i, n: (i, 0))),
        scratch_shapes=[pltpu.VMEM((tm, d), BF16)],
        compiler_params=_params(2), name=name)(x2, mod3, mod3, w, w_kr)


def _log_sigmoid(v):
    return jnp.minimum(v, 0.0) - jnp.log1p(jnp.exp(-jnp.abs(v)))


def _retention_kernel(q_ref, k_ref, v_ref, g_ref, kc_ref, vc_ref, cos_ref, slo_ref, shi_ref,
                      df_ref, db_ref, gn_ref, o_ref, kv_ref, st_ref, kr_ref, *, seq, ctx_len, chunk):
    n_chunks = seq // chunk
    lg_f = _log_sigmoid(df_ref[0])[0:1, :]
    lg_b = _log_sigmoid(db_ref[0])[0:1, :]
    lg_f_k, lg_b_k = lg_f[:, :RET_DK], lg_b[:, :RET_DK]

    def key_decay(length):
        j = lax.broadcasted_iota(jnp.int32, (length, RET_DK), 0).astype(F32)
        return jnp.exp((length - 1.0 - j) * lg_f_k), jnp.exp(j * lg_b_k)

    def state_update(kf, kd_f, kd_b, v):
        kk = jnp.concatenate([kf * kd_f, kf * kd_b], axis=1).astype(BF16)
        return lax.dot_general(kk, v, TN_DIMS, preferred_element_type=F32)

    kd_f, kd_b = key_decay(chunk)
    rows = [slice(c * chunk, (c + 1) * chunk) for c in range(n_chunks)]

    def rope_rows(ref, r):
        return _rope(ref[r, :].astype(F32), cos_ref[r, :], slo_ref[r, :], shi_ref[r, :], RET_DK // 4)

    for c, r in enumerate(rows):
        kf = rope_rows(k_ref, r)
        kr_ref[r, :] = kf.astype(BF16)
        kv_ref[c] = state_update(kf, kd_f, kd_b, v_ref[r, :])

    cd_f, cd_b = key_decay(ctx_len)
    ctx_state = state_update(kc_ref[...].astype(F32), cd_f, cd_b, vc_ref[...])
    g_chunk_f = jnp.exp(chunk * lg_f)
    g_chunk_b = jnp.exp(chunk * lg_b)
    r_f = ctx_state[:RET_DK]
    for c in range(n_chunks):
        st_ref[c, :RET_DK, :] = r_f.astype(BF16)
        r_f = r_f * g_chunk_f + kv_ref[c, :RET_DK, :]
    r_b = ctx_state[RET_DK:]
    for c in reversed(range(n_chunks)):
        st_ref[c, RET_DK:, :] = r_b.astype(BF16)
        r_b = r_b * g_chunk_b + kv_ref[c, RET_DK:, :]

    i = lax.broadcasted_iota(jnp.int32, (chunk, chunk), 0)
    j = lax.broadcasted_iota(jnp.int32, (chunk, chunk), 1)
    diff = (i - j).astype(F32)
    lower, upper = diff >= 0, diff <= 0
    decay = (jnp.where(lower, jnp.exp(jnp.where(lower, diff, 0.0) * lg_f), 0.0)
             + jnp.where(upper, jnp.exp(jnp.where(upper, -diff, 0.0) * lg_b), 0.0))
    qi = lax.broadcasted_iota(jnp.int32, (chunk, RET_DK), 0).astype(F32)
    qd_f = jnp.exp((qi + 1.0) * lg_f_k)
    qd_b = jnp.exp((chunk - qi) * lg_b_k)
    gain = gn_ref[...]

    def scores(r):
        qf = rope_rows(q_ref, r) * RET_DK ** -0.5
        s = lax.dot_general(qf.astype(BF16), kr_ref[r, :], NT_DIMS, preferred_element_type=F32) * decay
        return s.astype(BF16), jnp.concatenate([qf * qd_f, qf * qd_b], axis=1).astype(BF16)

    def outputs(c, r, s, qq):
        o = (jnp.dot(s, v_ref[r, :], preferred_element_type=F32)
             + jnp.dot(qq, st_ref[c], preferred_element_type=F32))
        gate = g_ref[r, :].astype(F32)
        o_ref[r, :] = (_rms(o) * gain * (gate * _sigmoid(gate))).astype(BF16)

    nxt = scores(rows[0])
    for c, r in enumerate(rows):
        cur = nxt
        if c + 1 < n_chunks:
            nxt = scores(rows[c + 1])
        outputs(c, r, *cur)


def _retention(p, pc, tabs, dec_f, dec_b, gn, batch, seq, ctx_len, col):
    chunk = RET_CHUNK
    assert seq % chunk == 0 and col["rk"] % RET_DK == 0 and col["rq"] % RET_DK == 0
    assert col["rv"] % RET_DV == 0 and col["rg"] % RET_DV == 0
    kq, kk, kv, kg = col["rq"] // RET_DK, col["rk"] // RET_DK, col["rv"] // RET_DV, col["rg"] // RET_DV
    tab_spec = pl.BlockSpec((seq, LANES), lambda b, h: (0, 0))
    dec_spec = pl.BlockSpec((1, 8, chunk), lambda b, h: (h, 0, 0))
    return pl.pallas_call(
        functools.partial(_retention_kernel, seq=seq, ctx_len=ctx_len, chunk=chunk),
        out_shape=jax.ShapeDtypeStruct((batch * seq, RET_V_W), BF16),
        grid=(batch, RET_HEADS),
        in_specs=[pl.BlockSpec((seq, RET_DK), lambda b, h: (b, kq + h)),
                  pl.BlockSpec((seq, RET_DK), lambda b, h: (b, kk + h)),
                  pl.BlockSpec((seq, RET_DV), lambda b, h: (b, kv + h)),
                  pl.BlockSpec((seq, RET_DV), lambda b, h: (b, kg + h)),
                  pl.BlockSpec((ctx_len, RET_DK), lambda b, h: (b, kk + h)),
                  pl.BlockSpec((ctx_len, RET_DV), lambda b, h: (b, kv + h)),
                  tab_spec, tab_spec, tab_spec, dec_spec, dec_spec,
                  pl.BlockSpec((1, RET_DV), lambda b, h: (0, h))],
        out_specs=pl.BlockSpec((seq, RET_DV), lambda b, h: (b, h)),
        scratch_shapes=[pltpu.VMEM((seq // chunk, 2 * RET_DK, RET_DV), F32),
                        pltpu.VMEM((seq // chunk, 2 * RET_DK, RET_DV), BF16),
                        pltpu.VMEM((seq, RET_DK), BF16)],
        compiler_params=_params(2), name="retention")(p, p, p, p, pc, pc, *tabs, dec_f, dec_b, gn)


def _q_prep_kernel(qd_ref, nrm_ref, w_ref, cos_ref, slo_ref, shi_ref, o_ref, *, scale):
    qn = (_rms(qd_ref[...].astype(F32)) * nrm_ref[...]).astype(BF16)
    cos, slo, shi = cos_ref[...], slo_ref[...], shi_ref[...]
    for h in range(MLA_HEADS):
        lo = h * HEAD_QK_W
        acc = jnp.dot(qn, w_ref[:, lo:lo + HEAD_QK_W], preferred_element_type=F32)
        o_ref[:, lo:lo + LANES] = (acc[:, :LANES] * scale).astype(BF16)
        rot = _rope(acc[:, LANES:], cos, slo, shi, MLA_D_ROPE // 4)
        o_ref[:, lo + LANES:lo + HEAD_QK_W] = (rot * scale).astype(BF16)


def _q_prep(p, nrm, wq, tabs, seq, col, scale):
    m = p.shape[0]
    tm = _tile(seq, 512)
    assert col["qd"] % MLA_Q_RANK == 0
    kq = col["qd"] // MLA_Q_RANK
    n = MLA_HEADS * HEAD_QK_W
    tab_spec = pl.BlockSpec((tm, LANES), lambda i: (i % (seq // tm), 0))
    return pl.pallas_call(
        functools.partial(_q_prep_kernel, scale=scale),
        out_shape=jax.ShapeDtypeStruct((m, n), BF16),
        grid=(m // tm,),
        in_specs=[pl.BlockSpec((tm, MLA_Q_RANK), lambda i: (i, kq)),
                  pl.BlockSpec((1, MLA_Q_RANK), lambda i: (0, 0)),
                  pl.BlockSpec((MLA_Q_RANK, n), lambda i: (0, 0)),
                  tab_spec, tab_spec, tab_spec],
        out_specs=pl.BlockSpec((tm, n), lambda i: (i, 0)),
        compiler_params=_params(1), name="q_prep")(p, nrm, wq, *tabs)


def _kv_prep_kernel(*refs, rope):
    if rope:
        kvd_ref, kr_ref, nrm_ref, wk_ref, wvt_ref, cos_ref, slo_ref, shi_ref, k_ref, vt_ref = refs
    else:
        kvd_ref, kr_ref, nrm_ref, wk_ref, wvt_ref, k_ref, vt_ref = refs
    kvn = (_rms(kvd_ref[...].astype(F32)) * nrm_ref[...]).astype(BF16)
    kr = kr_ref[...].astype(F32)
    if rope:
        kr = _rope(kr, cos_ref[...], slo_ref[...], shi_ref[...], MLA_D_ROPE // 4)
    kr = kr.astype(BF16)
    for pair in range(MLA_HEADS // 2):
        acc = jnp.dot(kvn, wk_ref[:, pair * 2 * LANES:(pair + 1) * 2 * LANES], preferred_element_type=F32)
        for half in range(2):
            lo = (2 * pair + half) * HEAD_QK_W
            k_ref[:, lo:lo + LANES] = acc[:, half * LANES:(half + 1) * LANES].astype(BF16)
            k_ref[:, lo + LANES:lo + HEAD_QK_W] = kr
    vt_ref[...] = lax.dot_general(wvt_ref[...], kvn, NT_DIMS, preferred_element_type=F32).astype(BF16)


def _kv_prep(p, kr, nrm, wk, wvt, tabs, seq, col, name):
    m = p.shape[0]
    tm = _tile(seq, 512)
    nt = seq // tm
    assert col["kvd"] % MLA_KV_RANK == 0
    kk = col["kvd"] // MLA_KV_RANK
    nk, nv = MLA_HEADS * HEAD_QK_W, MLA_HEADS * MLA_D_V
    in_specs = [pl.BlockSpec((tm, MLA_KV_RANK), lambda i: (i, kk)),
                pl.BlockSpec((tm, LANES), lambda i: (i, 0)),
                pl.BlockSpec((1, MLA_KV_RANK), lambda i: (0, 0)),
                pl.BlockSpec((MLA_KV_RANK, MLA_HEADS * MLA_D_NOPE), lambda i: (0, 0)),
                pl.BlockSpec((nv, MLA_KV_RANK), lambda i: (0, 0))]
    args = [p, kr, nrm, wk, wvt]
    if tabs is not None:
        in_specs += [pl.BlockSpec((tm, LANES), lambda i: (i % nt, 0))] * 3
        args += list(tabs)
    return pl.pallas_call(
        functools.partial(_kv_prep_kernel, rope=tabs is not None),
        out_shape=(jax.ShapeDtypeStruct((m, nk), BF16), jax.ShapeDtypeStruct((m // seq * nv, seq), BF16)),
        grid=(m // tm,),
        in_specs=in_specs,
        out_specs=(pl.BlockSpec((tm, nk), lambda i: (i, 0)), pl.BlockSpec((nv, tm), lambda i: (i // nt, i % nt))),
        compiler_params=_params(1), name=name)(*args)


def _attn_kernel(q_ref, kl_ref, kc_ref, vl_ref, vc_ref, o_ref, s_ref, p_ref, *, tq):
    lc = kc_ref.shape[0]
    n_tiles = q_ref.shape[0] // tq

    def scores(i):
        q = q_ref[i * tq:(i + 1) * tq, :]
        s_ref[i % 2, :lc, :] = lax.dot_general(kc_ref[...], q, NT_DIMS, preferred_element_type=F32)
        s_ref[i % 2, lc:, :] = lax.dot_general(kl_ref[...], q, NT_DIMS, preferred_element_type=F32)

    def softmax_pv(i):
        s = s_ref[i % 2]
        p = jnp.exp2(s - jnp.max(s, axis=0, keepdims=True))
        denom = jnp.sum(p, axis=0, keepdims=True)
        p_ref[i % 2] = p.astype(BF16)
        o_t = (jnp.dot(vc_ref[...], p_ref[i % 2, :lc, :], preferred_element_type=F32)
               + jnp.dot(vl_ref[...], p_ref[i % 2, lc:, :], preferred_element_type=F32))
        o_ref[i * tq:(i + 1) * tq, :] = (o_t / denom).T.astype(BF16)

    scores(0)
    for i in range(n_tiles):
        if i + 1 < n_tiles:
            scores(i + 1)
        softmax_pv(i)


def _attention(q, k_lat, k_ctx, vt_lat, vt_ctx, batch, seq, ctx_len):
    tq = _tile(seq, 512)
    return pl.pallas_call(
        functools.partial(_attn_kernel, tq=tq),
        out_shape=jax.ShapeDtypeStruct((batch * seq, MLA_HEADS * MLA_D_V), BF16),
        grid=(batch, MLA_HEADS),
        in_specs=[pl.BlockSpec((seq, HEAD_QK_W), lambda b, h: (b, h)),
                  pl.BlockSpec((seq, HEAD_QK_W), lambda b, h: (b, h)),
                  pl.BlockSpec((ctx_len, HEAD_QK_W), lambda b, h: (b, h)),
                  pl.BlockSpec((MLA_D_V, seq), lambda b, h: (b * MLA_HEADS + h, 0)),
                  pl.BlockSpec((MLA_D_V, ctx_len), lambda b, h: (b * MLA_HEADS + h, 0))],
        out_specs=pl.BlockSpec((seq, MLA_D_V), lambda b, h: (b, h)),
        scratch_shapes=[pltpu.VMEM((2, ctx_len + seq, tq), F32), pltpu.VMEM((2, ctx_len + seq, tq), BF16)],
        compiler_params=_params(2), name="attention")(q, k_lat, k_ctx, vt_lat, vt_ctx)


def _merge_kernel(ret_ref, att_ref, wr_ref, wa_ref, gr_ref, ga_ref, o_ref):
    r = jnp.dot(ret_ref[...], wr_ref[...], preferred_element_type=F32)
    a = jnp.dot(att_ref[...], wa_ref[...], preferred_element_type=F32)
    o_ref[...] = (_sigmoid(gr_ref[...].astype(F32)) * r + _sigmoid(ga_ref[...].astype(F32)) * a).astype(BF16)


def _merge(o_ret, o_att, w_ret, w_att, p, col):
    m, d = o_ret.shape
    n = w_ret.shape[1]
    tm, tn = _tile(m, 1024), _tile(n, 512)
    assert col["gate_ret"] % tn == 0 and col["gate_mla"] % tn == 0
    kr, ka = col["gate_ret"] // tn, col["gate_mla"] // tn
    return pl.pallas_call(
        _merge_kernel,
        out_shape=jax.ShapeDtypeStruct((m, n), BF16),
        grid=(m // tm, n // tn),
        in_specs=[pl.BlockSpec((tm, d), lambda i, j: (i, 0)),
                  pl.BlockSpec((tm, d), lambda i, j: (i, 0)),
                  pl.BlockSpec((d, tn), lambda i, j: (0, j)),
                  pl.BlockSpec((d, tn), lambda i, j: (0, j)),
                  pl.BlockSpec((tm, tn), lambda i, j: (i, kr + j)),
                  pl.BlockSpec((tm, tn), lambda i, j: (i, ka + j))],
        out_specs=pl.BlockSpec((tm, tn), lambda i, j: (i, j)),
        compiler_params=_params(2), name="merge")(o_ret, o_att, w_ret, w_att, p, p)


def _out_proj_kernel(mx_ref, w_ref, x_ref, g_ref, sh_ref, sc_ref, x1_ref, h2_ref):
    y = jnp.dot(mx_ref[...], w_ref[...], preferred_element_type=F32)
    x1 = x_ref[...] + g_ref[0] * y
    x1_ref[...] = x1
    h2_ref[...] = (_rms(x1) * (1.0 + sc_ref[0]) + sh_ref[0]).astype(BF16)


def _out_proj(mixed, w, x2, mod3, seq):
    m, d = x2.shape
    tm = _tile(seq, 512)
    mod_idx = lambda j: (lambda i: (i * tm // seq, 0, j))
    return pl.pallas_call(
        _out_proj_kernel,
        out_shape=(jax.ShapeDtypeStruct((m, d), F32), jax.ShapeDtypeStruct((m, d), BF16)),
        grid=(m // tm,),
        in_specs=[pl.BlockSpec((tm, d), lambda i: (i, 0)),
                  pl.BlockSpec((d, d), lambda i: (0, 0)),
                  pl.BlockSpec((tm, d), lambda i: (i, 0)),
                  pl.BlockSpec((1, 1, d), mod_idx(2)),
                  pl.BlockSpec((1, 1, d), mod_idx(3)),
                  pl.BlockSpec((1, 1, d), mod_idx(4))],
        out_specs=(pl.BlockSpec((tm, d), lambda i: (i, 0)), pl.BlockSpec((tm, d), lambda i: (i, 0))),
        compiler_params=_params(1), name="out_proj")(mixed, w, x2, mod3, mod3, mod3)


def _ffn_up_kernel(h_ref, wa_ref, wv_ref, cw_ref, cb_ref, o_ref, a_ref, v_ref, *, sub):
    rows = h_ref.shape[0]
    n_sub = o_ref.shape[1] // sub

    def matmuls(j):
        cs = slice(j * sub, (j + 1) * sub)
        a_ref[j % 2] = jnp.dot(h_ref[...], wa_ref[:, cs], preferred_element_type=F32)
        v_ref[j % 2] = jnp.dot(h_ref[...], wv_ref[:, cs], preferred_element_type=F32)

    def epilogue(j):
        cs = slice(j * sub, (j + 1) * sub)
        a = a_ref[j % 2]
        row = lax.broadcasted_iota(jnp.int32, a.shape, 0)
        a_prev = jnp.where(row == 0, 0.0, pltpu.roll(a, 1, 0))
        a_next = jnp.where(row == rows - 1, 0.0, pltpu.roll(a, rows - 1, 0))
        conv = cb_ref[:, cs] + a_prev * cw_ref[0:1, cs] + a * cw_ref[1:2, cs] + a_next * cw_ref[2:3, cs]
        o_ref[:, cs] = (conv * _sigmoid(conv) * v_ref[j % 2]).astype(BF16)

    matmuls(0)
    for j in range(n_sub):
        if j + 1 < n_sub:
            matmuls(j + 1)
        epilogue(j)


def _ffn_up(h2, w_up, conv_w, conv_b, batch, seq):
    d = h2.shape[1]
    f = conv_w.shape[1]
    tn = _tile(f, 512)
    sub = _tile(tn, 256)
    nf = f // tn
    return pl.pallas_call(
        functools.partial(_ffn_up_kernel, sub=sub),
        out_shape=jax.ShapeDtypeStruct((batch * seq, f), BF16),
        grid=(batch, nf),
        scratch_shapes=[pltpu.VMEM((2, seq, sub), F32), pltpu.VMEM((2, seq, sub), F32)],
        in_specs=[pl.BlockSpec((seq, d), lambda b, j: (b, 0)),
                  pl.BlockSpec((d, tn), lambda b, j: (0, j)),
                  pl.BlockSpec((d, tn), lambda b, j: (0, nf + j)),
                  pl.BlockSpec((conv_w.shape[0], tn), lambda b, j: (0, j)),
                  pl.BlockSpec((1, tn), lambda b, j: (0, j))],
        out_specs=pl.BlockSpec((seq, tn), lambda b, j: (b, j)),
        compiler_params=_params(2), name="ffn_up")(h2, w_up, w_up, conv_w, conv_b)


def _ffn_down_kernel(g_ref, w_ref, x1_ref, g2_ref, fn_ref, o_ref, y_ref):
    n = pl.program_id(1)
    y_ref[n] = jnp.dot(g_ref[...], w_ref[...], preferred_element_type=F32)

    @pl.when(n == pl.num_programs(1) - 1)
    def _():
        n_tiles, _, tn = y_ref.shape
        cols = [slice(j * tn, (j + 1) * tn) for j in range(n_tiles)]
        g2 = g2_ref[0]
        x2 = [x1_ref[:, cs] + g2[:, cs] * y_ref[j] for j, cs in enumerate(cols)]
        ssq = sum(jnp.sum(v * v, axis=-1, keepdims=True) for v in x2)
        inv = lax.rsqrt(ssq / (n_tiles * tn) + EPS)
        for v, cs in zip(x2, cols):
            o_ref[:, cs] = v * inv * fn_ref[:, cs]


def _ffn_down(g, w, x1, mod3, fn, seq):
    m, f = g.shape
    d = w.shape[1]
    tm, tn = _tile(seq, 512), _tile(d, 512)
    return pl.pallas_call(
        _ffn_down_kernel,
        out_shape=jax.ShapeDtypeStruct((m, d), F32),
        grid=(m // tm, d // tn),
        in_specs=[pl.BlockSpec((tm, f), lambda i, n: (i, 0)),
                  pl.BlockSpec((f, tn), lambda i, n: (0, n)),
                  pl.BlockSpec((tm, d), lambda i, n: (i, 0)),
                  pl.BlockSpec((1, 1, d), lambda i, n: (i * tm // seq, 0, 5)),
                  pl.BlockSpec((1, d), lambda i, n: (0, 0))],
        out_specs=pl.BlockSpec((tm, d), lambda i, n: (i, 0)),
        scratch_shapes=[pltpu.VMEM((d // tn, tm, tn), F32)],
        compiler_params=_params(2), name="ffn_down")(g, w, x1, mod3, fn)


def kernel(x, c, ctx, c_ctx, w_ada, b_ada, w_in, ret_decay_fwd, ret_decay_bwd, ret_gn, w_ret_o, mla_q_norm, w_q_up, mla_kv_norm, w_kv_up, w_mla_o, w_out, ffn_w_up, ffn_conv_w, ffn_conv_b, ffn_w_down, final_norm):
    b, s, d = x.shape
    lc = ctx.shape[1]
    assert w_ada.shape[0] == 1, "single trunk layer"
    assert s % GRID_W == 0

    n_mod = -(-(b + 1) // 8) * 8
    cs = jnp.concatenate([c, c_ctx[None], jnp.zeros((n_mod - b - 1, d), F32)], axis=0)
    mod3 = _ada(cs, w_ada[0], b_ada).reshape(n_mod, 1, 6 * d)

    src, off = {}, 0
    for name, width in zip(("rk", "rv", "kvd", "kr", "rq", "rg", "qd", "gate_ret", "gate_mla"),
                           (RET_QK_W, RET_V_W, MLA_KV_RANK, MLA_D_ROPE, RET_QK_W, RET_V_W, MLA_Q_RANK, d, d)):
        src[name] = (off, width)
        off += width
    order = ("rk", "rv", "kvd", "rq", "rg", "qd", "gate_ret", "gate_mla")
    col, off = {}, 0
    for name in order:
        col[name] = off
        off += src[name][1]
    n_ctx_cols = col["rq"]
    w_in0 = w_in[0]
    w_main = jnp.concatenate([w_in0[:, src[n][0]:src[n][0] + src[n][1]] for n in order], axis=1).astype(BF16)
    w_kr = jnp.pad(w_in0[:, src["kr"][0]:src["kr"][0] + MLA_D_ROPE], ((0, 0), (0, LANES - MLA_D_ROPE))).astype(BF16)

    p, kr = _in_proj(x.reshape(b * s, d), mod3, 0, s, w_main, w_kr, off, "in_proj")
    pc, krc = _in_proj(ctx.reshape(b * lc, d), mod3, b, b * lc, w_main, w_kr, n_ctx_cols, "in_proj_ctx")

    dec_f = jnp.broadcast_to(ret_decay_fwd[0].astype(F32)[:, None, None], (RET_HEADS, 8, RET_CHUNK))
    dec_b = jnp.broadcast_to(ret_decay_bwd[0].astype(F32)[:, None, None], (RET_HEADS, 8, RET_CHUNK))
    o_ret = _retention(p, pc, _rope_tables(s, RET_DK), dec_f, dec_b, ret_gn, b, s, lc, col)

    dqk = MLA_D_NOPE + MLA_D_ROPE
    wq3 = w_q_up[0].reshape(MLA_Q_RANK, MLA_HEADS, dqk)
    wq = jnp.pad(wq3, ((0, 0), (0, 0), (0, HEAD_QK_W - dqk))).reshape(MLA_Q_RANK, MLA_HEADS * HEAD_QK_W).astype(BF16)
    wkv3 = w_kv_up[0].reshape(MLA_KV_RANK, MLA_HEADS, MLA_D_NOPE + MLA_D_V)
    wk = wkv3[:, :, :MLA_D_NOPE].reshape(MLA_KV_RANK, MLA_HEADS * MLA_D_NOPE).astype(BF16)
    wvt = wkv3[:, :, MLA_D_NOPE:].reshape(MLA_KV_RANK, MLA_HEADS * MLA_D_V).T.astype(BF16)
    tabs = _rope_tables(s, MLA_D_ROPE)
    q = _q_prep(p, mla_q_norm, wq, tabs, s, col, dqk ** -0.5 * float(np.log2(np.e)))
    k_lat, vt_lat = _kv_prep(p, kr, mla_kv_norm, wk, wvt, tabs, s, col, "kv_prep")
    k_ctx, vt_ctx = _kv_prep(pc, krc, mla_kv_norm, wk, wvt, None, lc, col, "kv_prep_ctx")
    o_att = _attention(q, k_lat, k_ctx, vt_lat, vt_ctx, b, s, lc)

    mixed = _merge(o_ret, o_att, w_ret_o[0].astype(BF16), w_mla_o[0].astype(BF16), p, col)
    x1, h2 = _out_proj(mixed, w_out[0].astype(BF16), x.reshape(b * s, d), mod3, s)

    g = _ffn_up(h2, ffn_w_up[0].astype(BF16), ffn_conv_w[0], ffn_conv_b, b, s)
    out = _ffn_down(g, ffn_w_down[0].astype(BF16), x1, mod3, final_norm[None], s)
    return out.reshape(b, s, d)
```
